```python
import jax, jax.numpy as jnp
from jax import lax
import numpy as np

D_MODEL = 2048
BATCH = 4
SEQ = 8192
DEPTH = 1
DEC_BATCH = 2
DEC_SEQ = 16384
PAST_LEN = 128

EPS = 1e-6
N_MEM = 256
G_WIDTH = 2048
G_GROUPS = 8
G_CH = G_WIDTH // G_GROUPS
CHUNK = 128
N_HEADS = 16
Q_LORA = 512
KV_LORA = 512
D_NOPE = 128
D_ROPE = 64
D_V = 128
D_QK = D_NOPE + D_ROPE
ROPE_THETA = 10000.0
Q_BLOCK = 128
X_HEADS = 4
X_HEAD_DIM = 128
MOE_GROUPS = 8
EXP_PER_GROUP = 8
N_EXPERTS = MOE_GROUPS * EXP_PER_GROUP
TOP_K = 2
D_EXPERT = 1408
MOE_BLOCK = 128
SPLITS = [G_WIDTH, 2 * G_WIDTH, 2 * G_WIDTH + Q_LORA, 2 * G_WIDTH + Q_LORA + KV_LORA,
          2 * G_WIDTH + Q_LORA + KV_LORA + D_ROPE, 2 * G_WIDTH + Q_LORA + KV_LORA + D_ROPE + D_MODEL]
D_IN = 2 * G_WIDTH + Q_LORA + KV_LORA + D_ROPE + 2 * D_MODEL

kernel_name = 'hybrid_gmlp_mla_hiermoe_encoder'


def rmsnorm(x, g):
    xf = x.astype(jnp.float32)
    y = xf * lax.rsqrt(jnp.mean(xf * xf, axis=-1, keepdims=True) + EPS)
    return (y * g.astype(jnp.float32)).astype(x.dtype)


def layernorm(x, g, b):
    xf = x.astype(jnp.float32)
    mu = jnp.mean(xf, axis=-1, keepdims=True)
    xc = xf - mu
    var = jnp.mean(xc * xc, axis=-1, keepdims=True)
    return (xc * lax.rsqrt(var + EPS) * g.astype(jnp.float32) + b.astype(jnp.float32)).astype(x.dtype)


def rope_tables(S):
    inv = ROPE_THETA ** (-jnp.arange(0, D_ROPE, 2, dtype=jnp.float32) / D_ROPE)
    ang = jnp.arange(S, dtype=jnp.float32)[:, None] * inv[None, :]
    return jnp.cos(ang), jnp.sin(ang)


def apply_rope(x, cos, sin):
    x1, x2 = jnp.split(x.astype(jnp.float32), 2, axis=-1)
    return jnp.concatenate([x1 * cos - x2 * sin, x2 * cos + x1 * sin], axis=-1).astype(x.dtype)


def gmlp_branch(u, v, ln_g, ln_b, w_s, b_s):
    B, S, _ = v.shape
    vn = layernorm(v, ln_g, ln_b).reshape(B, S // CHUNK, CHUNK, G_GROUPS, G_CH)
    mixed = jnp.einsum('gpq,bnqgc->bnpgc', w_s, vn) + b_s.T[None, None, :, :, None]
    return u * mixed.reshape(B, S, G_WIDTH)


def mla_branch(cq, ckv, kr, q_norm_g, w_uq, kv_norm_g, w_ukv):
    B, S, _ = cq.shape
    q = (rmsnorm(cq, q_norm_g) @ w_uq).reshape(B, S, N_HEADS, D_QK)
    q_nope, q_rope = q[..., :D_NOPE], q[..., D_NOPE:]
    kv = (rmsnorm(ckv, kv_norm_g) @ w_ukv).reshape(B, S, N_HEADS, D_NOPE + D_V)
    k_nope, v = kv[..., :D_NOPE], kv[..., D_NOPE:]
    cos, sin = rope_tables(S)
    q_rope = apply_rope(q_rope, cos[:, None, :], sin[:, None, :])
    k_rope = apply_rope(kr, cos, sin)
    scale = D_QK ** -0.5
    nq = S // Q_BLOCK
    qn_b = q_nope.reshape(B, nq, Q_BLOCK, N_HEADS, D_NOPE).transpose(1, 0, 2, 3, 4)
    qr_b = q_rope.reshape(B, nq, Q_BLOCK, N_HEADS, D_ROPE).transpose(1, 0, 2, 3, 4)

    def attend_block(args):
        qn, qr = args
        s = (jnp.einsum('bqhd,bkhd->bhqk', qn, k_nope)
             + jnp.einsum('bqhr,bkr->bhqk', qr, k_rope))
        p = jax.nn.softmax(s.astype(jnp.float32) * scale, axis=-1).astype(v.dtype)
        return jnp.einsum('bhqk,bkhd->bqhd', p, v)

    o = lax.map(attend_block, (qn_b, qr_b))
    return o.transpose(1, 0, 2, 3, 4).reshape(B, S, N_HEADS * D_V)


def memory_xattn(h, mem, g_mem, w_xq, w_xkv, w_xo):
    B, S, _ = h.shape
    m = rmsnorm(mem, g_mem)
    q = (h @ w_xq).reshape(B, S, X_HEADS, X_HEAD_DIM)
    kv = (m @ w_xkv).reshape(B, m.shape[1], 2, X_HEADS, X_HEAD_DIM)
    k, v = kv[:, :, 0], kv[:, :, 1]
    s = jnp.einsum('bqhd,bkhd->bhqk', q, k).astype(jnp.float32) * (X_HEAD_DIM ** -0.5)
    p = jax.nn.softmax(s, axis=-1).astype(v.dtype)
    o = jnp.einsum('bhqk,bkhd->bqhd', p, v).reshape(B, S, X_HEADS * X_HEAD_DIM)
    return o @ w_xo


def grouped_experts(t, e_idx, gate, w1, w3, w2):
    T, D = t.shape
    A = e_idx.shape[0]
    tok = jnp.arange(A, dtype=jnp.int32) // TOP_K
    order = jnp.argsort(e_idx)
    e_sorted = e_idx[order]
    counts = jnp.bincount(e_idx, length=N_EXPERTS)
    padded = (counts + MOE_BLOCK - 1) // MOE_BLOCK * MOE_BLOCK
    start = jnp.cumsum(counts) - counts
    pend = jnp.cumsum(padded)
    pstart = pend - padded
    dest = pstart[e_sorted] + (jnp.arange(A, dtype=jnp.int32) - start[e_sorted])
    n_rows = (-(-A // MOE_BLOCK) + N_EXPERTS) * MOE_BLOCK
    row_tok = jnp.full((n_rows,), T, jnp.int32).at[dest].set(tok[order])
    row_gate = jnp.zeros((n_rows,), gate.dtype).at[dest].set(gate[order])
    n_blk = n_rows // MOE_BLOCK
    blk_exp = jnp.clip(jnp.searchsorted(pend, jnp.arange(n_blk, dtype=jnp.int32) * MOE_BLOCK, side='right'),
                       0, N_EXPERTS - 1)
    t_pad = jnp.concatenate([t, jnp.zeros((1, D), t.dtype)], axis=0)
    xb = t_pad[row_tok].reshape(n_blk, MOE_BLOCK, D)

    def expert_block(args):
        xblk, e = args
        return (jax.nn.silu(xblk @ w1[e]) * (xblk @ w3[e])) @ w2[e]

    yb = lax.map(expert_block, (xb, blk_exp)).reshape(n_rows, D)
    out = jnp.zeros((T + 1, D), t.dtype).at[row_tok].add(yb * row_gate[:, None].astype(t.dtype))
    return out[:T]


def hier_moe(h, w_rg, b_rg, w_re, b_re, w1, w3, w2):
    B, S, D = h.shape
    T = B * S
    t = h.reshape(T, D)
    g_prob = jax.nn.softmax((t @ w_rg).astype(jnp.float32) + b_rg.astype(jnp.float32), axis=-1)
    grp = jnp.argmax(g_prob, axis=-1).astype(jnp.int32)
    g_gate = jnp.take_along_axis(g_prob, grp[:, None], axis=-1)
    e_logits = ((t @ w_re).astype(jnp.float32) + b_re.astype(jnp.float32)).reshape(T, MOE_GROUPS, EXP_PER_GROUP)
    e_in = jnp.take_along_axis(e_logits, grp[:, None, None], axis=1)[:, 0]
    top_v, top_i = lax.top_k(e_in, TOP_K)
    e_gate = jax.nn.softmax(top_v, axis=-1) * g_gate
    expert = grp[:, None] * EXP_PER_GROUP + top_i.astype(jnp.int32)
    y = grouped_experts(t, expert.reshape(-1), e_gate.reshape(-1), w1, w3, w2)
    return y.reshape(B, S, D)


def encoder_layer(x, mem, g_mix, w_in, sgu_ln_g, sgu_ln_b, w_spatial, b_spatial,
                  q_norm_g, w_uq, kv_norm_g, w_ukv, w_lift_a, w_lift_b, w_out,
                  g_xattn, g_mem, w_xq, w_xkv, w_xo,
                  g_moe, w_rg, b_rg, w_re, b_re, w1, w3, w2):
    h = rmsnorm(x, g_mix)
    z = h @ w_in
    u, v, cq, ckv, kr, ga, gb = jnp.split(z, SPLITS, axis=-1)
    y_a = gmlp_branch(jax.nn.gelu(u), jax.nn.gelu(v), sgu_ln_g, sgu_ln_b, w_spatial, b_spatial) @ w_lift_a
    y_b = mla_branch(cq, ckv, kr, q_norm_g, w_uq, kv_norm_g, w_ukv) @ w_lift_b
    merged = jax.nn.sigmoid(ga) * y_a + jax.nn.sigmoid(gb) * y_b
    x = x + merged @ w_out
    x = x + memory_xattn(rmsnorm(x, g_xattn), mem, g_mem, w_xq, w_xkv, w_xo)
    x = x + hier_moe(rmsnorm(x, g_moe), w_rg, b_rg, w_re, b_re, w1, w3, w2)
    return x


def setup_inputs(seed: int = 0) -> dict:
    key = jax.random.key(seed)
    ks = jax.random.split(key, 40)
    f32 = jnp.float32

    def nrm(k, shape, scale):
        return jax.random.normal(k, shape, f32) * scale

    def gain(k, shape):
        return 1.0 + 0.02 * jax.random.normal(k, shape, f32)

    L = DEPTH
    return {
        'x_prompt': nrm(ks[0], (BATCH, SEQ, D_MODEL), 1.0),
        'x_sample': nrm(ks[1], (DEC_BATCH, DEC_SEQ, D_MODEL), 1.0),
        'mem_prompt': nrm(ks[2], (BATCH, N_MEM, D_MODEL), 1.0),
        'mem_sample': nrm(ks[3], (DEC_BATCH, N_MEM, D_MODEL), 1.0),
        'g_mix': gain(ks[4], (L, D_MODEL)),
        'w_in': nrm(ks[5], (L, D_MODEL, D_IN), D_MODEL ** -0.5),
        'sgu_ln_g': gain(ks[6], (L, G_WIDTH)),
        'sgu_ln_b': nrm(ks[7], (L, G_WIDTH), 0.02),
        'w_spatial': nrm(ks[8], (L, G_GROUPS, CHUNK, CHUNK), CHUNK ** -0.5),
        'b_spatial': nrm(ks[9], (L, G_GROUPS, CHUNK), 0.02),
        'q_norm_g': gain(ks[10], (L, Q_LORA)),
        'w_uq': nrm(ks[11], (L, Q_LORA, N_HEADS * D_QK), Q_LORA ** -0.5),
        'kv_norm_g': gain(ks[12], (L, KV_LORA)),
        'w_ukv': nrm(ks[13], (L, KV_LORA, N_HEADS * (D_NOPE + D_V)), KV_LORA ** -0.5),
        'w_lift_a': nrm(ks[14], (L, G_WIDTH, D_MODEL), G_WIDTH ** -0.5),
        'w_lift_b': nrm(ks[15], (L, N_HEADS * D_V, D_MODEL), (N_HEADS * D_V) ** -0.5),
        'w_out': nrm(ks[16], (L, D_MODEL, D_MODEL), D_MODEL ** -0.5),
        'g_xattn': gain(ks[17], (L, D_MODEL)),
        'g_mem': gain(ks[18], (L, D_MODEL)),
        'w_xq': nrm(ks[19], (L, D_MODEL, X_HEADS * X_HEAD_DIM), D_MODEL ** -0.5),
        'w_xkv': nrm(ks[20], (L, D_MODEL, 2 * X_HEADS * X_HEAD_DIM), D_MODEL ** -0.5),
        'w_xo': nrm(ks[21], (L, X_HEADS * X_HEAD_DIM, D_MODEL), (X_HEADS * X_HEAD_DIM) ** -0.5),
        'g_moe': gain(ks[22], (L, D_MODEL)),
        'w_router_group': nrm(ks[23], (L, D_MODEL, MOE_GROUPS), D_MODEL ** -0.5),
        'b_router_group': nrm(ks[24], (L, MOE_GROUPS), 0.01),
        'w_router_expert': nrm(ks[25], (L, D_MODEL, N_EXPERTS), D_MODEL ** -0.5),
        'b_router_expert': nrm(ks[26], (L, N_EXPERTS), 0.01),
        'w_gate_e': nrm(ks[27], (L, N_EXPERTS, D_MODEL, D_EXPERT), D_MODEL ** -0.5),
        'w_up_e': nrm(ks[28], (L, N_EXPERTS, D_MODEL, D_EXPERT), D_MODEL ** -0.5),
        'w_down_e': nrm(ks[29], (L, N_EXPERTS, D_EXPERT, D_MODEL), D_EXPERT ** -0.5),
        'g_final': gain(ks[30], (D_MODEL,)),
    }


def reference(x_prompt, x_sample, mem_prompt, mem_sample, g_mix, w_in, sgu_ln_g, sgu_ln_b,
              w_spatial, b_spatial, q_norm_g, w_uq, kv_norm_g, w_ukv, w_lift_a, w_lift_b, w_out,
              g_xattn, g_mem, w_xq, w_xkv, w_xo, g_moe, w_router_group, b_router_group,
              w_router_expert, b_router_expert, w_gate_e, w_up_e, w_down_e, g_final):
    def trunk(x, mem):
        for l in range(DEPTH):
            x = encoder_layer(x, mem, g_mix[l], w_in[l], sgu_ln_g[l], sgu_ln_b[l], w_spatial[l], b_spatial[l],
                              q_norm_g[l], w_uq[l], kv_norm_g[l], w_ukv[l], w_lift_a[l], w_lift_b[l], w_out[l],
                              g_xattn[l], g_mem[l], w_xq[l], w_xkv[l], w_xo[l],
                              g_moe[l], w_router_group[l], b_router_group[l], w_router_expert[l],
                              b_router_expert[l], w_gate_e[l], w_up_e[l], w_down_e[l])
        return rmsnorm(x, g_final)

    y_prompt = trunk(x_prompt, mem_prompt)
    y_sample = trunk(x_sample, mem_sample)
    return (y_prompt, y_sample)
```

```python
import functools

import jax
import jax.numpy as jnp
from jax import lax
from jax.experimental import pallas as pl
from jax.experimental.pallas import tpu as pltpu

F32 = jnp.float32
BF16 = jnp.bfloat16

EPS = 1e-6
G_GROUPS = 8
CHUNK = 128
N_HEADS = 16
Q_LORA = 512
KV_LORA = 512
D_NOPE = 128
D_ROPE = 64
D_V = 128
D_QK = D_NOPE + D_ROPE
ROPE_THETA = 10000.0
X_HEADS = 4
X_HEAD_DIM = 128
MOE_GROUPS = 8
EXP_PER_GROUP = 8
TOP_K = 2

LANES = 128
D_HEAD_PAD = 2 * LANES
MOE_BLK = 256
VMEM_LIMIT = 56 * 1024 * 1024

_NT = (((1,), (1,)), ((), ()))


def _params(*sem):
    return pltpu.CompilerParams(dimension_semantics=sem, vmem_limit_bytes=VMEM_LIMIT)


def _rms(x, g):
    return x * lax.rsqrt(jnp.mean(x * x, axis=-1, keepdims=True) + EPS) * g


def _norm_matmul_kernel(x_ref, g_ref, w_ref, o_ref, h_ref):
    @pl.when(pl.program_id(1) == 0)
    def _():
        h_ref[...] = _rms(x_ref[...], g_ref[...]).astype(BF16)

    o_ref[...] = jnp.dot(h_ref[...], w_ref[...], preferred_element_type=F32).astype(o_ref.dtype)


def _norm_matmul(x, g, w, *, tm, tn, out_dtype, name):
    t, d = x.shape
    n = w.shape[1]
    tm, tn = min(tm, t), min(tn, n)
    return pl.pallas_call(
        _norm_matmul_kernel,
        out_shape=jax.ShapeDtypeStruct((t, n), out_dtype),
        grid=(t // tm, n // tn),
        in_specs=[pl.BlockSpec((tm, d), lambda i, j: (i, 0)),
                  pl.BlockSpec((1, d), lambda i, j: (0, 0)),
                  pl.BlockSpec((d, tn), lambda i, j: (0, j))],
        out_specs=pl.BlockSpec((tm, tn), lambda i, j: (i, j)),
        scratch_shapes=[pltpu.VMEM((tm, d), BF16)],
        compiler_params=_params("parallel", "arbitrary"),
        name=name,
    )(x, g, w)


def _q_proj_kernel(c_ref, g_ref, w_ref, rope_ref, o_ref, *, scale):
    qn = _rms(c_ref[...], g_ref[...]).astype(BF16)
    q = jnp.dot(qn, w_ref[...], preferred_element_type=F32)
    rope = rope_ref[...]
    for h in range(N_HEADS):
        b = h * D_HEAD_PAD
        o_ref[:, b:b + LANES] = (q[:, b:b + LANES] * scale).astype(BF16)
        r = q[:, b + LANES:b + 2 * LANES] * rope
        r = r + pltpu.roll(r, D_ROPE, 1)
        o_ref[:, b + LANES:b + 2 * LANES] = (r * scale).astype(BF16)


def _q_proj(c, g, w, rope, *, seq, tm, scale):
    t = c.shape[0]
    tm = min(tm, seq)
    n = w.shape[1]
    per_seq = seq // tm
    return pl.pallas_call(
        functools.partial(_q_proj_kernel, scale=scale),
        out_shape=jax.ShapeDtypeStruct((t, n), BF16),
        grid=(t // tm,),
        in_specs=[pl.BlockSpec((tm, Q_LORA), lambda i: (i, 0)),
                  pl.BlockSpec((1, Q_LORA), lambda i: (0, 0)),
                  pl.BlockSpec((Q_LORA, n), lambda i: (0, 0)),
                  pl.BlockSpec((tm, LANES), lambda i: (i % per_seq, 0))],
        out_specs=pl.BlockSpec((tm, n), lambda i: (i, 0)),
        compiler_params=_params("parallel"),
        name="mla_q_proj",
    )(c, g, w, rope)


def _kv_proj_kernel(ckv_ref, kr_ref, g_ref, w_ref, rope_ref, k_ref, v_ref):
    kvn = _rms(ckv_ref[...], g_ref[...]).astype(BF16)
    kv = jnp.dot(kvn, w_ref[...], preferred_element_type=F32)
    r = kr_ref[...] * rope_ref[...]
    r = r + pltpu.roll(r, D_ROPE, 1)
    lane = lax.broadcasted_iota(jnp.int32, r.shape, 1)
    r = jnp.where(lane < D_ROPE, r, 0.0).astype(BF16)
    for h in range(N_HEADS):
        b = h * D_HEAD_PAD
        k_ref[:, b:b + LANES] = kv[:, b:b + LANES].astype(BF16)
        k_ref[:, b + LANES:b + 2 * LANES] = r
        v_ref[:, h * D_V:(h + 1) * D_V] = kv[:, b + LANES:b + 2 * LANES].astype(BF16)


def _kv_proj(c, g, w, rope, *, seq, tm):
    t = c.shape[0]
    tm = min(tm, seq)
    n = w.shape[1]
    per_seq = seq // tm
    return pl.pallas_call(
        _kv_proj_kernel,
        out_shape=(jax.ShapeDtypeStruct((t, N_HEADS * D_HEAD_PAD), BF16),
                   jax.ShapeDtypeStruct((t, N_HEADS * D_V), BF16)),
        grid=(t // tm,),
        in_specs=[pl.BlockSpec((tm, KV_LORA), lambda i: (i, Q_LORA // KV_LORA)),
                  pl.BlockSpec((tm, LANES), lambda i: (i, (Q_LORA + KV_LORA) // LANES)),
                  pl.BlockSpec((1, KV_LORA), lambda i: (0, 0)),
                  pl.BlockSpec((KV_LORA, n), lambda i: (0, 0)),
                  pl.BlockSpec((tm, LANES), lambda i: (i % per_seq, 0))],
        out_specs=(pl.BlockSpec((tm, N_HEADS * D_HEAD_PAD), lambda i: (i, 0)),
                   pl.BlockSpec((tm, N_HEADS * D_V), lambda i: (i, 0))),
        compiler_params=_params("parallel"),
        name="mla_kv_proj",
    )(c, c, g, w, rope)


def _flash_kernel(q_ref, k_ref, v_ref, o_ref, m_ref, l_ref, acc_ref, *, tk, nk):
    m_ref[...] = jnp.full(m_ref.shape, -jnp.inf, F32)
    l_ref[...] = jnp.zeros(l_ref.shape, F32)
    acc_ref[...] = jnp.zeros(acc_ref.shape, F32)
    q = q_ref[...]

    def body(j, carry):
        off = pl.multiple_of(j * tk, tk)
        k = k_ref[pl.ds(off, tk), :]
        v = v_ref[pl.ds(off, tk), :]
        s = lax.dot_general(q, k, _NT, preferred_element_type=F32)
        m_prev = m_ref[...]
        m_new = jnp.maximum(m_prev, jnp.max(s, axis=1, keepdims=True))
        p = jnp.exp(s - m_new)
        alpha = jnp.exp(m_prev - m_new)
        l_ref[...] = alpha * l_ref[...] + jnp.sum(p, axis=1, keepdims=True)
        acc_ref[...] = alpha * acc_ref[...] + jnp.dot(p.astype(BF16), v, preferred_element_type=F32)
        m_ref[...] = m_new
        return carry

    lax.fori_loop(0, nk, body, 0)
    o_ref[...] = (acc_ref[...] / l_ref[...]).astype(o_ref.dtype)


def _flash(q, k, v, *, batch, seq, tq, tk):
    t = q.shape[0]
    tq, tk = min(tq, seq), min(tk, seq)
    nq = seq // tq
    return pl.pallas_call(
        functools.partial(_flash_kernel, tk=tk, nk=seq // tk),
        out_shape=jax.ShapeDtypeStruct((t, N_HEADS * D_V), BF16),
        grid=(batch, N_HEADS, nq),
        in_specs=[pl.BlockSpec((tq, D_HEAD_PAD), lambda b, h, i: (b * nq + i, h)),
                  pl.BlockSpec((seq, D_HEAD_PAD), lambda b, h, i: (b, h)),
                  pl.BlockSpec((seq, D_V), lambda b, h, i: (b, h))],
        out_specs=pl.BlockSpec((tq, D_V), lambda b, h, i: (b * nq + i, h)),
        scratch_shapes=[pltpu.VMEM((tq, 1), F32), pltpu.VMEM((tq, 1), F32), pltpu.VMEM((tq, D_V), F32)],
        compiler_params=_params("parallel", "parallel", "arbitrary"),
        name="mla_flash",
    )(q, k, v)


def _gmlp_kernel(u_ref, v_ref, ga_ref, lng_ref, lnb_ref, ws_ref, bs_ref, wl_ref, o_ref, a_ref):
    gv = jax.nn.gelu(v_ref[...])
    mu = jnp.mean(gv, axis=-1, keepdims=True)
    xc = gv - mu
    var = jnp.mean(xc * xc, axis=-1, keepdims=True)
    vn = (xc * lax.rsqrt(var + EPS) * lng_ref[...] + lnb_ref[...]).astype(BF16)
    gu = jax.nn.gelu(u_ref[...])
    tm, width = gu.shape
    gch = width // G_GROUPS
    for n in range(tm // CHUNK):
        rows = slice(n * CHUNK, (n + 1) * CHUNK)
        for g in range(G_GROUPS):
            cols = slice(g * gch, (g + 1) * gch)
            mixed = jnp.dot(ws_ref[g], vn[rows, cols], preferred_element_type=F32) + bs_ref[g]
            a_ref[rows, cols] = (gu[rows, cols] * mixed).astype(BF16)
    y = jnp.dot(a_ref[...], wl_ref[...], preferred_element_type=F32)
    o_ref[...] = jax.nn.sigmoid(ga_ref[...]) * y


def _gmlp(z, lng, lnb, ws, bs, wl, *, tm):
    t = z.shape[0]
    width = wl.shape[0]
    d = wl.shape[1]
    tm = min(tm, t)
    const2 = lambda i: (0, 0)
    const3 = lambda i: (0, 0, 0)
    return pl.pallas_call(
        _gmlp_kernel,
        out_shape=jax.ShapeDtypeStruct((t, d), F32),
        grid=(t // tm,),
        in_specs=[pl.BlockSpec((tm, width), lambda i: (i, 0)),
                  pl.BlockSpec((tm, width), lambda i: (i, 1)),
                  pl.BlockSpec((tm, d), lambda i: (i, 2)),
                  pl.BlockSpec((1, width), const2),
                  pl.BlockSpec((1, width), const2),
                  pl.BlockSpec(ws.shape, const3),
                  pl.BlockSpec(bs.shape, const3),
                  pl.BlockSpec(wl.shape, const2, pipeline_mode=pl.Buffered(1))],
        out_specs=pl.BlockSpec((tm, d), lambda i: (i, 0)),
        scratch_shapes=[pltpu.VMEM((tm, width), BF16)],
        compiler_params=_params("parallel"),
        name="gmlp",
    )(z, z, z, lng, lnb, ws, bs, wl)


def _merge_kernel(x_ref, ma_ref, gb_ref, att_ref, wlb_ref, wout_ref, o_ref):
    yb = jnp.dot(att_ref[...], wlb_ref[...], preferred_element_type=F32)
    merged = ma_ref[...] + jax.nn.sigmoid(gb_ref[...]) * yb
    o_ref[...] = x_ref[...] + jnp.dot(merged.astype(BF16), wout_ref[...], preferred_element_type=F32)


def _merge_out(x, ma, z, att, wlb, wout, *, tm):
    t, d = x.shape
    tm = min(tm, t)
    const2 = lambda i: (0, 0)
    row = lambda i: (i, 0)
    return pl.pallas_call(
        _merge_kernel,
        out_shape=jax.ShapeDtypeStruct((t, d), F32),
        grid=(t // tm,),
        in_specs=[pl.BlockSpec((tm, d), row),
                  pl.BlockSpec((tm, d), row),
                  pl.BlockSpec((tm, d), lambda i: (i, 3)),
                  pl.BlockSpec((tm, att.shape[1]), row),
                  pl.BlockSpec(wlb.shape, const2, pipeline_mode=pl.Buffered(1)),
                  pl.BlockSpec(wout.shape, const2, pipeline_mode=pl.Buffered(1))],
        out_specs=pl.BlockSpec((tm, d), row),
        compiler_params=_params("parallel"),
        name="merge_out",
    )(x, ma, z, att, wlb, wout)


def _xattn_kernel(x_ref, g_ref, wq_ref, kv_ref, wo_ref, o_ref, *, scale):
    x = x_ref[...]
    h = _rms(x, g_ref[...]).astype(BF16)
    q = jnp.dot(h, wq_ref[...], preferred_element_type=F32)
    kv = kv_ref[...]
    hw = X_HEADS * X_HEAD_DIM
    outs = []
    for hd in range(X_HEADS):
        cols = slice(hd * X_HEAD_DIM, (hd + 1) * X_HEAD_DIM)
        s = lax.dot_general(q[:, cols].astype(BF16), kv[:, cols], _NT, preferred_element_type=F32) * scale
        p = jnp.exp(s - jnp.max(s, axis=1, keepdims=True))
        p = p / jnp.sum(p, axis=1, keepdims=True)
        vh = kv[:, hw + hd * X_HEAD_DIM:hw + (hd + 1) * X_HEAD_DIM]
        outs.append(jnp.dot(p.astype(BF16), vh, preferred_element_type=F32).astype(BF16))
    o = jnp.concatenate(outs, axis=1)
    o_ref[...] = x + jnp.dot(o, wo_ref[...], preferred_element_type=F32)


def _xattn(x, g, wq, kv, wo, *, seq, n_mem, tm):
    t, d = x.shape
    tm = min(tm, seq)
    const2 = lambda i: (0, 0)
    return pl.pallas_call(
        functools.partial(_xattn_kernel, scale=X_HEAD_DIM ** -0.5),
        out_shape=jax.ShapeDtypeStruct((t, d), F32),
        grid=(t // tm,),
        in_specs=[pl.BlockSpec((tm, d), lambda i: (i, 0)),
                  pl.BlockSpec((1, d), const2),
                  pl.BlockSpec(wq.shape, const2),
                  pl.BlockSpec((n_mem, kv.shape[1]), lambda i: ((i * tm) // seq, 0)),
                  pl.BlockSpec(wo.shape, const2)],
        out_specs=pl.BlockSpec((tm, d), lambda i: (i, 0)),
        compiler_params=_params("parallel"),
        name="mem_xattn",
    )(x, g, wq, kv, wo)


def _router_kernel(x_ref, g_ref, wr_ref, br_ref, h_ref, r_ref):
    h = _rms(x_ref[...], g_ref[...])
    h_ref[...] = h.astype(BF16)
    logits = jnp.dot(h, wr_ref[...], precision=lax.Precision.HIGHEST, preferred_element_type=F32) + br_ref[...]
    lane = lax.broadcasted_iota(jnp.int32, logits.shape, 1)
    lane_f = lane.astype(F32)
    big = jnp.float32(1e9)
    neg = jnp.float32(-jnp.inf)

    def first_argmax(vals):
        top = jnp.max(vals, axis=1, keepdims=True)
        idx = jnp.min(jnp.where(vals == top, lane_f, big), axis=1, keepdims=True)
        return top, idx

    gl = jnp.where(lane < MOE_GROUPS, logits, neg)
    gmax, grp = first_argmax(gl)
    g_gate = 1.0 / jnp.sum(jnp.exp(gl - gmax), axis=1, keepdims=True)
    e_lo = MOE_GROUPS + grp * EXP_PER_GROUP
    el = jnp.where((lane_f >= e_lo) & (lane_f < e_lo + EXP_PER_GROUP), logits, neg)
    t1, i1 = first_argmax(el)
    t2, i2 = first_argmax(jnp.where(lane_f == i1, neg, el))
    e = jnp.exp(t2 - t1)
    w1 = 1.0 / (1.0 + e)
    w2 = e / (1.0 + e)
    out = jnp.where(lane == 0, i1 - MOE_GROUPS,
                    jnp.where(lane == 1, i2 - MOE_GROUPS,
                              jnp.where(lane == 2, w1 * g_gate,
                                        jnp.where(lane == 3, w2 * g_gate, 0.0))))
    r_ref[...] = out


def _router(x, g, wr, br, *, tm):
    t, d = x.shape
    tm = min(tm, t)
    const2 = lambda i: (0, 0)
    return pl.pallas_call(
        _router_kernel,
        out_shape=(jax.ShapeDtypeStruct((t, d), BF16), jax.ShapeDtypeStruct((t, LANES), F32)),
        grid=(t // tm,),
        in_specs=[pl.BlockSpec((tm, d), lambda i: (i, 0)),
                  pl.BlockSpec((1, d), const2),
                  pl.BlockSpec(wr.shape, const2),
                  pl.BlockSpec((1, LANES), const2)],
        out_specs=(pl.BlockSpec((tm, d), lambda i: (i, 0)), pl.BlockSpec((tm, LANES), lambda i: (i, 0))),
        compiler_params=_params("parallel"),
        name="moe_router",
    )(x, g, wr, br)


def _expert_kernel(be_ref, na_ref, x_ref, w1_ref, w3_ref, w2_ref, y_ref):
    del be_ref
    active = pl.program_id(0) < na_ref[0]

    @pl.when(active)
    def _():
        x = x_ref[...]
        h1 = jnp.dot(x, w1_ref[0], preferred_element_type=F32)
        h3 = jnp.dot(x, w3_ref[0], preferred_element_type=F32)
        a = (jax.nn.silu(h1) * h3).astype(BF16)
        y_ref[...] = jnp.dot(a, w2_ref[0], preferred_element_type=F32)

    @pl.when(jnp.logical_not(active))
    def _():
        y_ref[...] = jnp.zeros(y_ref.shape, y_ref.dtype)


def _experts(blk_exp, n_active, xs, w1, w3, w2):
    n_rows, d = xs.shape
    f = w1.shape[2]
    grid_spec = pltpu.PrefetchScalarGridSpec(
        num_scalar_prefetch=2,
        grid=(n_rows // MOE_BLK,),
        in_specs=[pl.BlockSpec((MOE_BLK, d), lambda i, be, na: (i, 0)),
                  pl.BlockSpec((1, d, f), lambda i, be, na: (be[i], 0, 0)),
                  pl.BlockSpec((1, d, f), lambda i, be, na: (be[i], 0, 0)),
                  pl.BlockSpec((1, f, d), lambda i, be, na: (be[i], 0, 0))],
        out_specs=pl.BlockSpec((MOE_BLK, d), lambda i, be, na: (i, 0)),
    )
    return pl.pallas_call(
        _expert_kernel,
        out_shape=jax.ShapeDtypeStruct((n_rows, d), F32),
        grid_spec=grid_spec,
        compiler_params=_params("arbitrary"),
        name="moe_experts",
    )(blk_exp, n_active, xs, w1, w3, w2)


def _combine_kernel(x_ref, y0_ref, y1_ref, r_ref, g_ref, o_ref):
    r = r_ref[...]
    x3 = x_ref[...] + r[:, 2:3] * y0_ref[...] + r[:, 3:4] * y1_ref[...]
    o_ref[...] = _rms(x3, g_ref[...])


def _combine(x, y0, y1, route, g, *, tm):
    t, d = x.shape
    tm = min(tm, t)
    row = lambda i: (i, 0)
    return pl.pallas_call(
        _combine_kernel,
        out_shape=jax.ShapeDtypeStruct((t, d), F32),
        grid=(t // tm,),
        in_specs=[pl.BlockSpec((tm, d), row), pl.BlockSpec((tm, d), row), pl.BlockSpec((tm, d), row),
                  pl.BlockSpec((tm, LANES), row), pl.BlockSpec((1, d), lambda i: (0, 0))],
        out_specs=pl.BlockSpec((tm, d), row),
        compiler_params=_params("parallel"),
        name="moe_combine_final_norm",
    )(x, y0, y1, route, g)


def _rope_table(seq):
    inv = ROPE_THETA ** (-jnp.arange(0, D_ROPE, 2, dtype=F32) / D_ROPE)
    ang = jnp.arange(seq, dtype=F32)[:, None] * inv[None, :]
    cos, sin = jnp.cos(ang), jnp.sin(ang)
    return jnp.concatenate([cos, cos, -sin, sin], axis=1)


def _swap_halves(w):
    half = w.shape[-1] // 2
    return jnp.concatenate([w[..., half:], w[..., :half]], axis=-1)


def _prep_weights(g_mix, w_in, sgu_ln_g, sgu_ln_b, w_spatial, b_spatial, q_norm_g, w_uq, kv_norm_g, w_ukv,
                  w_lift_a, w_lift_b, w_out, g_xattn, g_mem, w_xq, w_xkv, w_xo, g_moe, w_rg, b_rg, w_re, b_re,
                  w1, w3, w2, g_final):
    gw = w_lift_a.shape[0]
    d = w_in.shape[0]
    o = 0
    parts = {}
    for name, width in (("u", gw), ("v", gw), ("cq", Q_LORA), ("ckv", KV_LORA), ("kr", D_ROPE), ("ga", d), ("gb", d)):
        parts[name] = w_in[:, o:o + width]
        o += width
    w_uq3 = w_uq.reshape(Q_LORA, N_HEADS, D_QK)
    q_rope = w_uq3[..., D_NOPE:]
    n_route = MOE_GROUPS + MOE_GROUPS * EXP_PER_GROUP
    row = lambda v: v.reshape(1, -1)
    return dict(
        g_mix=row(g_mix),
        w_main=jnp.concatenate([parts["u"], parts["v"], parts["ga"], parts["gb"]], axis=1).astype(BF16),
        w_c=jnp.concatenate([parts["cq"], parts["ckv"], parts["kr"], _swap_halves(parts["kr"])], axis=1).astype(BF16),
        ln_g=row(sgu_ln_g), ln_b=row(sgu_ln_b),
        w_s=w_spatial.astype(BF16),
        b_s=jnp.broadcast_to(b_spatial[:, :, None], b_spatial.shape + (gw // G_GROUPS,)),
        g_q=row(q_norm_g),
        w_uq=jnp.concatenate([w_uq3[..., :D_NOPE], q_rope, _swap_halves(q_rope)], axis=-1)
        .reshape(Q_LORA, N_HEADS * D_HEAD_PAD).astype(BF16),
        g_kv=row(kv_norm_g), w_ukv=w_ukv.astype(BF16),
        w_la=w_lift_a.astype(BF16), w_lb=w_lift_b.astype(BF16), w_out=w_out.astype(BF16),
        g_xattn=row(g_xattn), g_mem=row(g_mem),
        w_xq=w_xq.astype(BF16), w_xkv=w_xkv.astype(BF16), w_xo=w_xo.astype(BF16),
        g_moe=row(g_moe),
        w_r=jnp.pad(jnp.concatenate([w_rg, w_re], axis=1), ((0, 0), (0, LANES - n_route))),
        b_r=jnp.pad(jnp.concatenate([b_rg, b_re]), (0, LANES - n_route)).reshape(1, LANES),
        w1=w1.astype(BF16), w3=w3.astype(BF16), w2=w2.astype(BF16),
        g_final=row(g_final),
    )


def _dispatch_plan(e_idx, n_experts):
    t = e_idx.shape[0]
    a = t * TOP_K
    e_flat = e_idx.reshape(-1)
    order = jnp.argsort(e_flat)
    e_sorted = e_flat[order]
    counts = jnp.bincount(e_flat, length=n_experts)
    padded = (counts + MOE_BLK - 1) // MOE_BLK * MOE_BLK
    start = jnp.cumsum(counts) - counts
    pend = jnp.cumsum(padded)
    pstart = pend - padded
    dest = (pstart[e_sorted] + (jnp.arange(a, dtype=jnp.int32) - start[e_sorted])).astype(jnp.int32)
    n_blk = -(-a // MOE_BLK) + n_experts
    row_tok = jnp.zeros((n_blk * MOE_BLK,), jnp.int32).at[dest].set((order // TOP_K).astype(jnp.int32))
    pos = jnp.zeros((a,), jnp.int32).at[order].set(dest).reshape(t, TOP_K)
    blk_exp = jnp.clip(jnp.searchsorted(pend, jnp.arange(n_blk, dtype=jnp.int32) * MOE_BLK, side="right"),
                       0, n_experts - 1).astype(jnp.int32)
    n_active = (pend[-1] // MOE_BLK).astype(jnp.int32).reshape(1)
    return row_tok, pos, blk_exp, n_active


def _trunk(x3d, mem3d, p):
    batch, seq, d = x3d.shape
    n_mem = mem3d.shape[1]
    x = x3d.reshape(batch * seq, d)
    mem = mem3d.reshape(batch * n_mem, d)

    z = _norm_matmul(x, p["g_mix"], p["w_main"], tm=1024, tn=1024, out_dtype=F32, name="in_proj_main")
    c = _norm_matmul(x, p["g_mix"], p["w_c"], tm=1024, tn=p["w_c"].shape[1], out_dtype=F32, name="in_proj_latent")

    rope = _rope_table(seq)
    q = _q_proj(c, p["g_q"], p["w_uq"], rope, seq=seq, tm=256, scale=D_QK ** -0.5)
    k, v = _kv_proj(c, p["g_kv"], p["w_ukv"], rope, seq=seq, tm=256)
    att = _flash(q, k, v, batch=batch, seq=seq, tq=512, tk=512)

    ma = _gmlp(z, p["ln_g"], p["ln_b"], p["w_s"], p["b_s"], p["w_la"], tm=256)
    x1 = _merge_out(x, ma, z, att, p["w_lb"], p["w_out"], tm=256)

    kv_mem = _norm_matmul(mem, p["g_mem"], p["w_xkv"], tm=512, tn=p["w_xkv"].shape[1], out_dtype=BF16,
                          name="mem_kv_proj")
    x2 = _xattn(x1, p["g_xattn"], p["w_xq"], kv_mem, p["w_xo"], seq=seq, n_mem=n_mem, tm=512)

    h, route = _router(x2, p["g_moe"], p["w_r"], p["b_r"], tm=512)
    e_idx = route[:, :TOP_K].astype(jnp.int32)
    row_tok, pos, blk_exp, n_active = _dispatch_plan(e_idx, p["w1"].shape[0])
    y = _experts(blk_exp, n_active, h[row_tok], p["w1"], p["w3"], p["w2"])
    out = _combine(x2, y[pos[:, 0]], y[pos[:, 1]], route, p["g_final"], tm=512)
    return out.reshape(batch, seq, d)


def kernel(x_prompt, x_sample, mem_prompt, mem_sample, g_mix, w_in, sgu_ln_g, sgu_ln_b, w_spatial, b_spatial,
           q_norm_g, w_uq, kv_norm_g, w_ukv, w_lift_a, w_lift_b, w_out, g_xattn, g_mem, w_xq, w_xkv, w_xo, g_moe,
           w_router_group, b_router_group, w_router_expert, b_router_expert, w_gate_e, w_up_e, w_down_e, g_final):
    assert g_mix.shape[0] == 1, "one encoder layer"
    p = _prep_weights(g_mix[0], w_in[0], sgu_ln_g[0], sgu_ln_b[0], w_spatial[0], b_spatial[0], q_norm_g[0],
                      w_uq[0], kv_norm_g[0], w_ukv[0], w_lift_a[0], w_lift_b[0], w_out[0], g_xattn[0], g_mem[0],
                      w_xq[0], w_xkv[0], w_xo[0], g_moe[0], w_router_group[0], b_router_group[0],
                      w_router_expert[0], b_router_expert[0], w_gate_e[0], w_up_e[0], w_down_e[0], g_final)
    return (_trunk(x_prompt, mem_prompt, p), _trunk(x_sample, mem_sample, p))
```

```python
import functools

import jax
import jax.numpy as jnp
from jax import lax
from jax.experimental import pallas as pl
from jax.experimental.pallas import tpu as pltpu

F32 = jnp.float32
BF16 = jnp.bfloat16

EPS = 1e-6
G_GROUPS = 8
CHUNK = 128
N_HEADS = 16
Q_LORA = 512
KV_LORA = 512
D_NOPE = 128
D_ROPE = 64
D_V = 128
D_QK = D_NOPE + D_ROPE
ROPE_THETA = 10000.0
X_HEADS = 4
X_HEAD_DIM = 128
MOE_GROUPS = 8
EXP_PER_GROUP = 8
TOP_K = 2

LANES = 128
D_HEAD_PAD = 2 * LANES
MOE_BLK = 256
VMEM_LIMIT = 56 * 1024 * 1024
LOG2E = 1.4426950408889634
L_ROWS = 16

_NT = (((1,), (1,)), ((), ()))


def _params(*sem):
    return pltpu.CompilerParams(dimension_semantics=sem, vmem_limit_bytes=VMEM_LIMIT)


def _rms(x, g):
    return x * lax.rsqrt(jnp.mean(x * x, axis=-1, keepdims=True) + EPS) * g


def _norm_matmul_kernel(x_ref, g_ref, w_ref, o_ref, h_ref):
    @pl.when(pl.program_id(1) == 0)
    def _():
        h_ref[...] = _rms(x_ref[...], g_ref[...]).astype(BF16)

    o_ref[...] = jnp.dot(h_ref[...], w_ref[...], preferred_element_type=F32).astype(o_ref.dtype)


def _norm_matmul(x, g, w, *, tm, tn, out_dtype, name):
    t, d = x.shape
    n = w.shape[1]
    tm, tn = min(tm, t), min(tn, n)
    return pl.pallas_call(
        _norm_matmul_kernel,
        out_shape=jax.ShapeDtypeStruct((t, n), out_dtype),
        grid=(t // tm, n // tn),
        in_specs=[pl.BlockSpec((tm, d), lambda i, j: (i, 0)),
                  pl.BlockSpec((1, d), lambda i, j: (0, 0)),
                  pl.BlockSpec((d, tn), lambda i, j: (0, j))],
        out_specs=pl.BlockSpec((tm, tn), lambda i, j: (i, j)),
        scratch_shapes=[pltpu.VMEM((tm, d), BF16)],
        compiler_params=_params("parallel", "arbitrary"),
        name=name,
    )(x, g, w)


def _q_proj_kernel(c_ref, g_ref, wt_ref, rope_ref, o_ref, *, scale):
    qn = _rms(c_ref[...], g_ref[...]).astype(BF16)
    qt = lax.dot_general(wt_ref[...], qn, _NT, preferred_element_type=F32)
    rope = rope_ref[...]
    for h in range(N_HEADS):
        b = h * D_HEAD_PAD
        o_ref[b:b + LANES, :] = (qt[b:b + LANES, :] * scale).astype(BF16)
        r = qt[b + LANES:b + 2 * LANES, :] * rope
        r = r + jnp.concatenate([r[D_ROPE:], r[:D_ROPE]], axis=0)
        o_ref[b + LANES:b + 2 * LANES, :] = (r * scale).astype(BF16)


def _q_proj(c, g, wt, rope_t, *, seq, tm, scale):
    t = c.shape[0]
    tm = min(tm, seq)
    n = wt.shape[0]
    per_seq = seq // tm
    return pl.pallas_call(
        functools.partial(_q_proj_kernel, scale=scale),
        out_shape=jax.ShapeDtypeStruct((n, t), BF16),
        grid=(t // tm,),
        in_specs=[pl.BlockSpec((tm, Q_LORA), lambda i: (i, 0)),
                  pl.BlockSpec((1, Q_LORA), lambda i: (0, 0)),
                  pl.BlockSpec((n, Q_LORA), lambda i: (0, 0)),
                  pl.BlockSpec((LANES, tm), lambda i: (0, i % per_seq))],
        out_specs=pl.BlockSpec((n, tm), lambda i: (0, i)),
        compiler_params=_params("parallel"),
        name="mla_q_proj",
    )(c, g, wt, rope_t)


def _kv_proj_kernel(ckv_ref, kr_ref, g_ref, wk_ref, wvt_ref, rope_ref, k_ref, vt_ref):
    kvn = _rms(ckv_ref[...], g_ref[...]).astype(BF16)
    kn = jnp.dot(kvn, wk_ref[...], preferred_element_type=F32)
    vt_ref[...] = lax.dot_general(wvt_ref[...], kvn, _NT, preferred_element_type=F32).astype(BF16)
    r = kr_ref[...] * rope_ref[...]
    r = r + pltpu.roll(r, D_ROPE, 1)
    lane = lax.broadcasted_iota(jnp.int32, r.shape, 1)
    r = jnp.where(lane < D_ROPE, r, 0.0).astype(BF16)
    for h in range(N_HEADS):
        b = h * D_HEAD_PAD
        k_ref[:, b:b + LANES] = kn[:, h * D_NOPE:(h + 1) * D_NOPE].astype(BF16)
        k_ref[:, b + LANES:b + 2 * LANES] = r


def _kv_proj(c, g, wk, wvt, rope, *, seq, tm):
    t = c.shape[0]
    tm = min(tm, seq)
    per_seq = seq // tm
    const2 = lambda i: (0, 0)
    return pl.pallas_call(
        _kv_proj_kernel,
        out_shape=(jax.ShapeDtypeStruct((t, N_HEADS * D_HEAD_PAD), BF16),
                   jax.ShapeDtypeStruct((N_HEADS * D_V, t), BF16)),
        grid=(t // tm,),
        in_specs=[pl.BlockSpec((tm, KV_LORA), lambda i: (i, Q_LORA // KV_LORA)),
                  pl.BlockSpec((tm, LANES), lambda i: (i, (Q_LORA + KV_LORA) // LANES)),
                  pl.BlockSpec((1, KV_LORA), const2),
                  pl.BlockSpec(wk.shape, const2),
                  pl.BlockSpec(wvt.shape, const2),
                  pl.BlockSpec((tm, LANES), lambda i: (i % per_seq, 0))],
        out_specs=(pl.BlockSpec((tm, N_HEADS * D_HEAD_PAD), lambda i: (i, 0)),
                   pl.BlockSpec((N_HEADS * D_V, tm), lambda i: (0, i))),
        compiler_params=_params("parallel"),
        name="mla_kv_proj",
    )(c, c, g, wk, wvt, rope)


def _flash_kernel(qt_ref, k_ref, vt_ref, o_ref, acc_ref, st_ref, *, tk, nk):
    qt = qt_ref[...]
    tq = qt.shape[1]
    acc_ref[...] = jnp.zeros(acc_ref.shape, F32)
    ones = jnp.ones((L_ROWS, tk), BF16)

    def scores(j):
        off = pl.multiple_of(j * tk, tk)
        st = jnp.dot(k_ref[pl.ds(off, tk), :], qt, preferred_element_type=F32)
        return st, jnp.max(st, axis=0, keepdims=True)

    st0, c0 = scores(0)
    st_ref[...] = st0

    def body(j, carry):
        m, c = carry
        m_new = jnp.maximum(m, c)
        p = jnp.exp2(st_ref[...] - m_new).astype(BF16)
        alpha = jnp.exp2(m - m_new)
        st_n, c_n = scores(jnp.minimum(j + 1, nk - 1))
        off = pl.multiple_of(j * tk, tk)
        vt = jnp.concatenate([vt_ref[:, pl.ds(off, tk)], ones], axis=0)
        acc_ref[...] = alpha * acc_ref[...] + jnp.dot(vt, p, preferred_element_type=F32)
        st_ref[...] = st_n
        return m_new, c_n

    lax.fori_loop(0, nk, body, (jnp.full((1, tq), -jnp.inf, F32), c0))
    o_ref[...] = (acc_ref[:D_V, :] / acc_ref[D_V:D_V + 1, :]).T.astype(o_ref.dtype)


def _flash(qt, k, vt, *, batch, seq, tq, tk):
    t = k.shape[0]
    tq, tk = min(tq, seq), min(tk, seq)
    nq, nk = seq // tq, seq // tk
    return pl.pallas_call(
        functools.partial(_flash_kernel, tk=tk, nk=nk),
        out_shape=jax.ShapeDtypeStruct((t, N_HEADS * D_V), BF16),
        grid=(batch, N_HEADS, nq),
        in_specs=[pl.BlockSpec((D_HEAD_PAD, tq), lambda b, h, i: (h, b * nq + i)),
                  pl.BlockSpec((seq, D_HEAD_PAD), lambda b, h, i: (b, h)),
                  pl.BlockSpec((D_V, seq), lambda b, h, i: (h, b))],
        out_specs=pl.BlockSpec((tq, D_V), lambda b, h, i: (b * nq + i, h)),
        scratch_shapes=[pltpu.VMEM((D_V + L_ROWS, tq), F32), pltpu.VMEM((tk, tq), F32)],
        compiler_params=_params("parallel", "parallel", "arbitrary"),
        name="mla_flash",
    )(qt, k, vt)


def _gmlp_kernel(u_ref, v_ref, ga_ref, lng_ref, lnb_ref, ws_ref, bs_ref, wl_ref, o_ref, a_ref):
    gv = jax.nn.gelu(v_ref[...])
    mu = jnp.mean(gv, axis=-1, keepdims=True)
    xc = gv - mu
    var = jnp.mean(xc * xc, axis=-1, keepdims=True)
    vn = (xc * lax.rsqrt(var + EPS) * lng_ref[...] + lnb_ref[...]).astype(BF16)
    gu = jax.nn.gelu(u_ref[...])
    tm, width = gu.shape
    gch = width // G_GROUPS
    for n in range(tm // CHUNK):
        rows = slice(n * CHUNK, (n + 1) * CHUNK)
        for g in range(G_GROUPS):
            cols = slice(g * gch, (g + 1) * gch)
            mixed = jnp.dot(ws_ref[g], vn[rows, cols], preferred_element_type=F32) + bs_ref[g]
            a_ref[rows, cols] = (gu[rows, cols] * mixed).astype(BF16)
    y = jnp.dot(a_ref[...], wl_ref[...], preferred_element_type=F32)
    o_ref[...] = jax.nn.sigmoid(ga_ref[...]) * y


def _gmlp(z, lng, lnb, ws, bs, wl, *, tm):
    t = z.shape[0]
    width = wl.shape[0]
    d = wl.shape[1]
    tm = min(tm, t)
    const2 = lambda i: (0, 0)
    const3 = lambda i: (0, 0, 0)
    return pl.pallas_call(
        _gmlp_kernel,
        out_shape=jax.ShapeDtypeStruct((t, d), F32),
        grid=(t // tm,),
        in_specs=[pl.BlockSpec((tm, width), lambda i: (i, 0)),
                  pl.BlockSpec((tm, width), lambda i: (i, 1)),
                  pl.BlockSpec((tm, d), lambda i: (i, 2)),
                  pl.BlockSpec((1, width), const2),
                  pl.BlockSpec((1, width), const2),
                  pl.BlockSpec(ws.shape, const3),
                  pl.BlockSpec(bs.shape, const3),
                  pl.BlockSpec(wl.shape, const2, pipeline_mode=pl.Buffered(1))],
        out_specs=pl.BlockSpec((tm, d), lambda i: (i, 0)),
        scratch_shapes=[pltpu.VMEM((tm, width), BF16)],
        compiler_params=_params("parallel"),
        name="gmlp",
    )(z, z, z, lng, lnb, ws, bs, wl)


def _merge_kernel(x_ref, ma_ref, gb_ref, att_ref, wlb_ref, wout_ref, o_ref):
    yb = jnp.dot(att_ref[...], wlb_ref[...], preferred_element_type=F32)
    merged = ma_ref[...] + jax.nn.sigmoid(gb_ref[...]) * yb
    o_ref[...] = x_ref[...] + jnp.dot(merged.astype(BF16), wout_ref[...], preferred_element_type=F32)


def _merge_out(x, ma, z, att, wlb, wout, *, tm):
    t, d = x.shape
    tm = min(tm, t)
    const2 = lambda i: (0, 0)
    row = lambda i: (i, 0)
    return pl.pallas_call(
        _merge_kernel,
        out_shape=jax.ShapeDtypeStruct((t, d), F32),
        grid=(t // tm,),
        in_specs=[pl.BlockSpec((tm, d), row),
                  pl.BlockSpec((tm, d), row),
                  pl.BlockSpec((tm, d), lambda i: (i, 3)),
                  pl.BlockSpec((tm, att.shape[1]), row),
                  pl.BlockSpec(wlb.shape, const2, pipeline_mode=pl.Buffered(1)),
                  pl.BlockSpec(wout.shape, const2, pipeline_mode=pl.Buffered(1))],
        out_specs=pl.BlockSpec((tm, d), row),
        compiler_params=_params("parallel"),
        name="merge_out",
    )(x, ma, z, att, wlb, wout)


def _xattn_kernel(x_ref, g_ref, wq_ref, kv_ref, wo_ref, o_ref, *, scale):
    x = x_ref[...]
    h = _rms(x, g_ref[...]).astype(BF16)
    q = jnp.dot(h, wq_ref[...], preferred_element_type=F32)
    kv = kv_ref[...]
    hw = X_HEADS * X_HEAD_DIM
    outs = []
    for hd in range(X_HEADS):
        cols = slice(hd * X_HEAD_DIM, (hd + 1) * X_HEAD_DIM)
        s = lax.dot_general(q[:, cols].astype(BF16), kv[:, cols], _NT, preferred_element_type=F32) * scale
        p = jnp.exp(s - jnp.max(s, axis=1, keepdims=True))
        p = p / jnp.sum(p, axis=1, keepdims=True)
        vh = kv[:, hw + hd * X_HEAD_DIM:hw + (hd + 1) * X_HEAD_DIM]
        outs.append(jnp.dot(p.astype(BF16), vh, preferred_element_type=F32).astype(BF16))
    o = jnp.concatenate(outs, axis=1)
    o_ref[...] = x + jnp.dot(o, wo_ref[...], preferred_element_type=F32)


def _xattn(x, g, wq, kv, wo, *, seq, n_mem, tm):
    t, d = x.shape
    tm = min(tm, seq)
    const2 = lambda i: (0, 0)
    return pl.pallas_call(
        functools.partial(_xattn_kernel, scale=X_HEAD_DIM ** -0.5),
        out_shape=jax.ShapeDtypeStruct((t, d), F32),
        grid=(t // tm,),
        in_specs=[pl.BlockSpec((tm, d), lambda i: (i, 0)),
                  pl.BlockSpec((1, d), const2),
                  pl.BlockSpec(wq.shape, const2),
                  pl.BlockSpec((n_mem, kv.shape[1]), lambda i: ((i * tm) // seq, 0)),
                  pl.BlockSpec(wo.shape, const2)],
        out_specs=pl.BlockSpec((tm, d), lambda i: (i, 0)),
        compiler_params=_params("parallel"),
        name="mem_xattn",
    )(x, g, wq, kv, wo)


def _router_kernel(x_ref, g_ref, wr_ref, br_ref, h_ref, r_ref):
    h = _rms(x_ref[...], g_ref[...])
    h_ref[...] = h.astype(BF16)
    logits = jnp.dot(h, wr_ref[...], precision=lax.Precision.HIGHEST, preferred_element_type=F32) + br_ref[...]
    lane = lax.broadcasted_iota(jnp.int32, logits.shape, 1)
    lane_f = lane.astype(F32)
    big = jnp.float32(1e9)
    neg = jnp.float32(-jnp.inf)

    def first_argmax(vals):
        top = jnp.max(vals, axis=1, keepdims=True)
        idx = jnp.min(jnp.where(vals == top, lane_f, big), axis=1, keepdims=True)
        return top, idx

    gl = jnp.where(lane < MOE_GROUPS, logits, neg)
    gmax, grp = first_argmax(gl)
    g_gate = 1.0 / jnp.sum(jnp.exp(gl - gmax), axis=1, keepdims=True)
    e_lo = MOE_GROUPS + grp * EXP_PER_GROUP
    el = jnp.where((lane_f >= e_lo) & (lane_f < e_lo + EXP_PER_GROUP), logits, neg)
    t1, i1 = first_argmax(el)
    t2, i2 = first_argmax(jnp.where(lane_f == i1, neg, el))
    e = jnp.exp(t2 - t1)
    w1 = 1.0 / (1.0 + e)
    w2 = e / (1.0 + e)
    out = jnp.where(lane == 0, i1 - MOE_GROUPS,
                    jnp.where(lane == 1, i2 - MOE_GROUPS,
                              jnp.where(lane == 2, w1 * g_gate,
                                        jnp.where(lane == 3, w2 * g_gate, 0.0))))
    r_ref[...] = out


def _router(x, g, wr, br, *, tm):
    t, d = x.shape
    tm = min(tm, t)
    const2 = lambda i: (0, 0)
    return pl.pallas_call(
        _router_kernel,
        out_shape=(jax.ShapeDtypeStruct((t, d), BF16), jax.ShapeDtypeStruct((t, LANES), F32)),
        grid=(t // tm,),
        in_specs=[pl.BlockSpec((tm, d), lambda i: (i, 0)),
                  pl.BlockSpec((1, d), const2),
                  pl.BlockSpec(wr.shape, const2),
                  pl.BlockSpec((1, LANES), const2)],
        out_specs=(pl.BlockSpec((tm, d), lambda i: (i, 0)), pl.BlockSpec((tm, LANES), lambda i: (i, 0))),
        compiler_params=_params("parallel"),
        name="moe_router",
    )(x, g, wr, br)


def _expert_kernel(be_ref, na_ref, x_ref, w1_ref, w3_ref, w2_ref, y_ref):
    del be_ref
    active = pl.program_id(0) < na_ref[0]

    @pl.when(active)
    def _():
        x = x_ref[...]
        h1 = jnp.dot(x, w1_ref[0], preferred_element_type=F32)
        h3 = jnp.dot(x, w3_ref[0], preferred_element_type=F32)
        a = (jax.nn.silu(h1) * h3).astype(BF16)
        y_ref[...] = jnp.dot(a, w2_ref[0], preferred_element_type=F32)

    @pl.when(jnp.logical_not(active))
    def _():
        y_ref[...] = jnp.zeros(y_ref.shape, y_ref.dtype)


def _experts(blk_exp, n_active, xs, w1, w3, w2):
    n_rows, d = xs.shape
    f = w1.shape[2]
    grid_spec = pltpu.PrefetchScalarGridSpec(
        num_scalar_prefetch=2,
        grid=(n_rows // MOE_BLK,),
        in_specs=[pl.BlockSpec((MOE_BLK, d), lambda i, be, na: (i, 0)),
                  pl.BlockSpec((1, d, f), lambda i, be, na: (be[i], 0, 0)),
                  pl.BlockSpec((1, d, f), lambda i, be, na: (be[i], 0, 0)),
                  pl.BlockSpec((1, f, d), lambda i, be, na: (be[i], 0, 0))],
        out_specs=pl.BlockSpec((MOE_BLK, d), lambda i, be, na: (i, 0)),
    )
    return pl.pallas_call(
        _expert_kernel,
        out_shape=jax.ShapeDtypeStruct((n_rows, d), F32),
        grid_spec=grid_spec,
        compiler_params=_params("arbitrary"),
        name="moe_experts",
    )(blk_exp, n_active, xs, w1, w3, w2)


def _combine_kernel(x_ref, y0_ref, y1_ref, r_ref, g_ref, o_ref):
    r = r_ref[...]
    x3 = x_ref[...] + r[:, 2:3] * y0_ref[...] + r[:, 3:4] * y1_ref[...]
    o_ref[...] = _rms(x3, g_ref[...])


def _combine(x, y0, y1, route, g, *, tm):
    t, d = x.shape
    tm = min(tm, t)
    row = lambda i: (i, 0)
    return pl.pallas_call(
        _combine_kernel,
        out_shape=jax.ShapeDtypeStruct((t, d), F32),
        grid=(t // tm,),
        in_specs=[pl.BlockSpec((tm, d), row), pl.BlockSpec((tm, d), row), pl.BlockSpec((tm, d), row),
                  pl.BlockSpec((tm, LANES), row), pl.BlockSpec((1, d), lambda i: (0, 0))],
        out_specs=pl.BlockSpec((tm, d), row),
        compiler_params=_params("parallel"),
        name="moe_combine_final_norm",
    )(x, y0, y1, route, g)


def _rope_table(seq):
    inv = ROPE_THETA ** (-jnp.arange(0, D_ROPE, 2, dtype=F32) / D_ROPE)
    ang = jnp.arange(seq, dtype=F32)[:, None] * inv[None, :]
    cos, sin = jnp.cos(ang), jnp.sin(ang)
    return jnp.concatenate([cos, cos, -sin, sin], axis=1)


def _swap_halves(w):
    half = w.shape[-1] // 2
    return jnp.concatenate([w[..., half:], w[..., :half]], axis=-1)


def _prep_weights(g_mix, w_in, sgu_ln_g, sgu_ln_b, w_spatial, b_spatial, q_norm_g, w_uq, kv_norm_g, w_ukv,
                  w_lift_a, w_lift_b, w_out, g_xattn, g_mem, w_xq, w_xkv, w_xo, g_moe, w_rg, b_rg, w_re, b_re,
                  w1, w3, w2, g_final):
    gw = w_lift_a.shape[0]
    d = w_in.shape[0]
    o = 0
    parts = {}
    for name, width in (("u", gw), ("v", gw), ("cq", Q_LORA), ("ckv", KV_LORA), ("kr", D_ROPE), ("ga", d), ("gb", d)):
        parts[name] = w_in[:, o:o + width]
        o += width
    w_uq3 = w_uq.reshape(Q_LORA, N_HEADS, D_QK)
    q_rope = w_uq3[..., D_NOPE:]
    w_ukv3 = w_ukv.reshape(KV_LORA, N_HEADS, D_NOPE + D_V)
    n_route = MOE_GROUPS + MOE_GROUPS * EXP_PER_GROUP
    row = lambda v: v.reshape(1, -1)
    return dict(
        g_mix=row(g_mix),
        w_main=jnp.concatenate([parts["u"], parts["v"], parts["ga"], parts["gb"]], axis=1).astype(BF16),
        w_c=jnp.concatenate([parts["cq"], parts["ckv"], parts["kr"], _swap_halves(parts["kr"])], axis=1).astype(BF16),
        ln_g=row(sgu_ln_g), ln_b=row(sgu_ln_b),
        w_s=w_spatial.astype(BF16),
        b_s=jnp.broadcast_to(b_spatial[:, :, None], b_spatial.shape + (gw // G_GROUPS,)),
        g_q=row(q_norm_g),
        w_uq_t=jnp.concatenate([w_uq3[..., :D_NOPE], q_rope, _swap_halves(q_rope)], axis=-1)
        .reshape(Q_LORA, N_HEADS * D_HEAD_PAD).T.astype(BF16),
        g_kv=row(kv_norm_g),
        w_uk=w_ukv3[..., :D_NOPE].reshape(KV_LORA, N_HEADS * D_NOPE).astype(BF16),
        w_uv_t=w_ukv3[..., D_NOPE:].reshape(KV_LORA, N_HEADS * D_V).T.astype(BF16),
        w_la=w_lift_a.astype(BF16), w_lb=w_lift_b.astype(BF16), w_out=w_out.astype(BF16),
        g_xattn=row(g_xattn), g_mem=row(g_mem),
        w_xq=w_xq.astype(BF16), w_xkv=w_xkv.astype(BF16), w_xo=w_xo.astype(BF16),
        g_moe=row(g_moe),
        w_r=jnp.pad(jnp.concatenate([w_rg, w_re], axis=1), ((0, 0), (0, LANES - n_route))),
        b_r=jnp.pad(jnp.concatenate([b_rg, b_re]), (0, LANES - n_route)).reshape(1, LANES),
        w1=w1.astype(BF16), w3=w3.astype(BF16), w2=w2.astype(BF16),
        g_final=row(g_final),
    )


def _dispatch_plan(e_idx, n_experts):
    t = e_idx.shape[0]
    a = t * TOP_K
    e_flat = e_idx.reshape(-1)
    order = jnp.argsort(e_flat)
    e_sorted = e_flat[order]
    counts = jnp.bincount(e_flat, length=n_experts)
    padded = (counts + MOE_BLK - 1) // MOE_BLK * MOE_BLK
    start = jnp.cumsum(counts) - counts
    pend = jnp.cumsum(padded)
    pstart = pend - padded
    dest = (pstart[e_sorted] + (jnp.arange(a, dtype=jnp.int32) - start[e_sorted])).astype(jnp.int32)
    n_blk = -(-a // MOE_BLK) + n_experts
    row_tok = jnp.zeros((n_blk * MOE_BLK,), jnp.int32).at[dest].set((order // TOP_K).astype(jnp.int32))
    pos = jnp.zeros((a,), jnp.int32).at[order].set(dest).reshape(t, TOP_K)
    blk_exp = jnp.clip(jnp.searchsorted(pend, jnp.arange(n_blk, dtype=jnp.int32) * MOE_BLK, side="right"),
                       0, n_experts - 1).astype(jnp.int32)
    n_active = (pend[-1] // MOE_BLK).astype(jnp.int32).reshape(1)
    return row_tok, pos, blk_exp, n_active


def _trunk(x3d, mem3d, p):
    batch, seq, d = x3d.shape
    n_mem = mem3d.shape[1]
    x = x3d.reshape(batch * seq, d)
    mem = mem3d.reshape(batch * n_mem, d)

    z = _norm_matmul(x, p["g_mix"], p["w_main"], tm=1024, tn=1024, out_dtype=F32, name="in_proj_main")
    c = _norm_matmul(x, p["g_mix"], p["w_c"], tm=1024, tn=p["w_c"].shape[1], out_dtype=F32, name="in_proj_latent")

    rope = _rope_table(seq)
    qt = _q_proj(c, p["g_q"], p["w_uq_t"], rope.T, seq=seq, tm=256, scale=LOG2E * D_QK ** -0.5)
    k, vt = _kv_proj(c, p["g_kv"], p["w_uk"], p["w_uv_t"], rope, seq=seq, tm=256)
    att = _flash(qt, k, vt, batch=batch, seq=seq, tq=1024, tk=512)

    ma = _gmlp(z, p["ln_g"], p["ln_b"], p["w_s"], p["b_s"], p["w_la"], tm=256)
    x1 = _merge_out(x, ma, z, att, p["w_lb"], p["w_out"], tm=256)

    kv_mem = _norm_matmul(mem, p["g_mem"], p["w_xkv"], tm=512, tn=p["w_xkv"].shape[1], out_dtype=BF16,
                          name="mem_kv_proj")
    x2 = _xattn(x1, p["g_xattn"], p["w_xq"], kv_mem, p["w_xo"], seq=seq, n_mem=n_mem, tm=512)

    h, route = _router(x2, p["g_moe"], p["w_r"], p["b_r"], tm=512)
    e_idx = route[:, :TOP_K].astype(jnp.int32)
    row_tok, pos, blk_exp, n_active = _dispatch_plan(e_idx, p["w1"].shape[0])
    y = _experts(blk_exp, n_active, h[row_tok], p["w1"], p["w3"], p["w2"])
    out = _combine(x2, y[pos[:, 0]], y[pos[:, 1]], route, p["g_final"], tm=512)
    return out.reshape(batch, seq, d)


def kernel(x_prompt, x_sample, mem_prompt, mem_sample, g_mix, w_in, sgu_ln_g, sgu_ln_b, w_spatial, b_spatial,
           q_norm_g, w_uq, kv_norm_g, w_ukv, w_lift_a, w_lift_b, w_out, g_xattn, g_mem, w_xq, w_xkv, w_xo, g_moe,
           w_router_group, b_router_group, w_router_expert, b_router_expert, w_gate_e, w_up_e, w_down_e, g_final):
    assert g_mix.shape[0] == 1, "one encoder layer"
    p = _prep_weights(g_mix[0], w_in[0], sgu_ln_g[0], sgu_ln_b[0], w_spatial[0], b_spatial[0], q_norm_g[0],
                      w_uq[0], kv_norm_g[0], w_ukv[0], w_lift_a[0], w_lift_b[0], w_out[0], g_xattn[0], g_mem[0],
                      w_xq[0], w_xkv[0], w_xo[0], g_moe[0], w_router_group[0], b_router_group[0],
                      w_router_expert[0], b_router_expert[0], w_gate_e[0], w_up_e[0], w_down_e[0], g_final)
    return (_trunk(x_prompt, mem_prompt, p), _trunk(x_sample, mem_sample, p))
```
